```python
import math
import jax, jax.numpy as jnp
from jax import lax
import numpy as np

D_MODEL = 1024
BATCH = 32
SEQ = 256
DEPTH = 4
DEC_BATCH = 4
DEC_SEQ = 2048
PAST_LEN = 256

GRID_W = 64
N_MIXERS = 4
N_LAYERS_POOL = (DEPTH + 3) // 4
N_LAYERS_SSD = (DEPTH + 2) // 4
N_LAYERS_CONV = (DEPTH + 1) // 4
N_LAYERS_ATTN = DEPTH // 4
N_MOD = 6
EPS = 1e-6

POOL_WINDOWS = (2, 4, 8, 16)
POOL_GROUP = D_MODEL // len(POOL_WINDOWS)

SSD_D_INNER = 2 * D_MODEL
SSD_HEAD_DIM = 64
SSD_HEADS = SSD_D_INNER // SSD_HEAD_DIM
SSD_GROUPS = 8
SSD_STATE = 128
SSD_CONV = 3
SSD_CHUNK = 128
SSD_GN = SSD_GROUPS * SSD_STATE
SSD_CONV_DIM = SSD_D_INNER + 2 * SSD_GN
SSD_IN_DIM = SSD_D_INNER + SSD_CONV_DIM + 2 * SSD_HEADS

CONF_KERNEL = 31

ATTN_HEADS = 8
ATTN_KV_HEADS = 2
ATTN_HEAD_DIM = 128
ROPE_THETA = 10000.0
Q_BLOCK = 128

FFN_DIM = 2816
FFN_CONV = 3

kernel_name = 'hybrid_diffusion_prefix_trunk_step'


def rms_norm(x, g):
    xf = x.astype(jnp.float32)
    y = xf * lax.rsqrt(jnp.mean(xf * xf, axis=-1, keepdims=True) + EPS)
    return (y * g.astype(jnp.float32)).astype(x.dtype)


def layer_norm(x, g, b):
    xf = x.astype(jnp.float32)
    mu = jnp.mean(xf, axis=-1, keepdims=True)
    var = jnp.mean(jnp.square(xf - mu), axis=-1, keepdims=True)
    y = (xf - mu) * lax.rsqrt(var + EPS)
    return (y * g.astype(jnp.float32) + b.astype(jnp.float32)).astype(x.dtype)


def depthwise_conv(x, w, b):
    k = w.shape[0]
    left = (k - 1) // 2
    y = lax.conv_general_dilated(
        x, w.astype(x.dtype)[:, None, :], window_strides=(1,),
        padding=[(left, k - 1 - left)],
        dimension_numbers=('NWC', 'WIO', 'NWC'),
        feature_group_count=x.shape[-1])
    return y + b.astype(x.dtype)


def modulation(cond, w, b):
    m = jax.nn.silu(cond) @ w + b
    return m.reshape(m.shape[:-1] + (N_MOD, D_MODEL))


def modulate_in(x, mod, k, g):
    return rms_norm(x, g) * (1.0 + mod[:, None, 3 * k + 1]) + mod[:, None, 3 * k]


def gated_residual(x, out, mod, k, g):
    return x + mod[:, None, 3 * k + 2] * rms_norm(out, g)


def pool_mixer(h, w, scale):
    seq_len = h.shape[1]
    hf = h.astype(jnp.float32)
    cs = jnp.concatenate([jnp.zeros_like(hf[:, :1]), jnp.cumsum(hf, axis=1)], axis=1)
    t = jnp.arange(seq_len)
    outs = []
    for gi, win in enumerate(POOL_WINDOWS):
        sl = slice(gi * POOL_GROUP, (gi + 1) * POOL_GROUP)
        lo = jnp.clip(t - win // 2, 0, seq_len)
        hi = jnp.clip(t + win - win // 2, 0, seq_len)
        csg = cs[..., sl]
        mean = (csg[:, hi] - csg[:, lo]) / (hi - lo).astype(jnp.float32)[None, :, None]
        outs.append(mean - hf[..., sl])
    p = jnp.stack(outs, axis=2).astype(h.dtype)
    y = jnp.einsum('blgc,gcd->blgd', p, w).reshape(h.shape)
    return y * scale


def ssd_scan(x, dt, a, bm, cm, h0):
    bsz, seq_len = x.shape[:2]
    nc = seq_len // SSD_CHUNK
    r = SSD_HEADS // SSD_GROUPS
    xc = x.reshape(bsz, nc, SSD_CHUNK, SSD_GROUPS, r, SSD_HEAD_DIM)
    dtc = dt.reshape(bsz, nc, SSD_CHUNK, SSD_GROUPS, r)
    bc = bm.reshape(bsz, nc, SSD_CHUNK, SSD_GROUPS, SSD_STATE)
    cc = cm.reshape(bsz, nc, SSD_CHUNK, SSD_GROUPS, SSD_STATE)
    acs = jnp.cumsum(dtc * a.reshape(SSD_GROUPS, r), axis=2)
    pos = jnp.arange(SSD_CHUNK)
    lower = (pos[:, None] >= pos[None, :])[None, None, :, :, None, None]
    seg = acs[:, :, :, None] - acs[:, :, None, :]
    decay = jnp.exp(jnp.where(lower, seg, -jnp.inf))
    cb = jnp.einsum('bcign,bcjgn->bcijg', cc, bc)
    wts = cb[..., None] * decay * dtc[:, :, None]
    y_diag = jnp.einsum('bcijgr,bcjgrp->bcigrp', wts, xc)
    to_end = jnp.exp(acs[:, :, -1:] - acs) * dtc
    chunk_states = jnp.einsum('bcjgn,bcjgr,bcjgrp->bcgrpn', bc, to_end, xc)
    chunk_decay = jnp.exp(acs[:, :, -1])

    def step(hs, inp):
        s, dcy = inp
        return hs * dcy[..., None, None] + s, hs

    h_init = h0.astype(jnp.float32).reshape(bsz, SSD_GROUPS, r, SSD_HEAD_DIM, SSD_STATE)
    h_last, h_prev = lax.scan(step, h_init, (jnp.moveaxis(chunk_states, 1, 0),
                                             jnp.moveaxis(chunk_decay, 1, 0)))
    h_prev = jnp.moveaxis(h_prev, 0, 1)
    y_off = jnp.einsum('bcign,bcgrpn->bcigrp', cc, h_prev) * jnp.exp(acs)[..., None]
    y = (y_diag + y_off).reshape(bsz, seq_len, SSD_HEADS, SSD_HEAD_DIM)
    return y, h_last.reshape(bsz, SSD_HEADS, SSD_HEAD_DIM, SSD_STATE)


def ssd_mixer(h, h0, w_in, conv_w, conv_b, dt_bias, a_log, d_skip, norm_g, w_out):
    bsz, seq_len, _ = h.shape
    proj = h @ w_in
    z = proj[..., :SSD_D_INNER]
    xbc = jax.nn.silu(depthwise_conv(proj[..., SSD_D_INNER:SSD_D_INNER + SSD_CONV_DIM], conv_w, conv_b))
    dt_raw = proj[..., SSD_D_INNER + SSD_CONV_DIM:].reshape(bsz, seq_len, 2, SSD_HEADS).astype(jnp.float32)
    x = xbc[..., :SSD_D_INNER].reshape(bsz, seq_len, SSD_HEADS, SSD_HEAD_DIM).astype(jnp.float32)
    bm = xbc[..., SSD_D_INNER:SSD_D_INNER + SSD_GN].reshape(bsz, seq_len, SSD_GROUPS, SSD_STATE).astype(jnp.float32)
    cm = xbc[..., SSD_D_INNER + SSD_GN:].reshape(bsz, seq_len, SSD_GROUPS, SSD_STATE).astype(jnp.float32)
    dt = jax.nn.softplus(dt_raw + dt_bias.astype(jnp.float32))
    a = -jnp.exp(a_log.astype(jnp.float32))
    y_f, s_f = ssd_scan(x, dt[:, :, 0], a[0], bm, cm, h0[:, 0])
    flip = lambda t: jnp.flip(t, axis=1)
    y_b, s_b = ssd_scan(flip(x), flip(dt[:, :, 1]), a[1], flip(bm), flip(cm), h0[:, 1])
    skip = jnp.sum(d_skip.astype(jnp.float32), axis=0)[:, None]
    y = (y_f + flip(y_b) + x * skip).reshape(bsz, seq_len, SSD_D_INNER).astype(h.dtype)
    y = rms_norm(y * jax.nn.silu(z), norm_g)
    return y @ w_out, jnp.stack([s_f, s_b], axis=1)


def conformer_conv(h, w_in, b_in, dw, dw_b, ln_g, ln_b, w_out, b_out):
    u = h @ w_in + b_in
    u = u[..., :D_MODEL] * jax.nn.sigmoid(u[..., D_MODEL:])
    u = depthwise_conv(u, dw, dw_b)
    u = jax.nn.silu(layer_norm(u, ln_g, ln_b))
    return u @ w_out + b_out


def axial_rope_tables(seq_len):
    rows = seq_len // GRID_W
    row = jnp.repeat(jnp.arange(rows), GRID_W).astype(jnp.float32)
    col = jnp.tile(jnp.arange(GRID_W), rows).astype(jnp.float32)
    n_freq = ATTN_HEAD_DIM // 4
    inv = ROPE_THETA ** (-jnp.arange(n_freq, dtype=jnp.float32) / n_freq)
    ang = jnp.stack([row[:, None] * inv, col[:, None] * inv], axis=1)
    return jnp.cos(ang), jnp.sin(ang)


def axial_rope(x, cos, sin):
    n_freq = ATTN_HEAD_DIM // 4
    xs = x.astype(jnp.float32).reshape(x.shape[:-1] + (2, 2, n_freq))
    x1, x2 = xs[..., 0, :], xs[..., 1, :]
    cs, sn = cos[None, :, None], sin[None, :, None]
    out = jnp.stack([x1 * cs - x2 * sn, x2 * cs + x1 * sn], axis=-2)
    return out.reshape(x.shape).astype(x.dtype)


def attn_qkv(h, w_qkv, q_g, k_g):
    bsz, seq_len, _ = h.shape
    nq, nk = ATTN_HEADS * ATTN_HEAD_DIM, ATTN_KV_HEADS * ATTN_HEAD_DIM
    qkv = h @ w_qkv
    q = rms_norm(qkv[..., :nq].reshape(bsz, seq_len, ATTN_HEADS, ATTN_HEAD_DIM), q_g)
    k = rms_norm(qkv[..., nq:nq + nk].reshape(bsz, seq_len, ATTN_KV_HEADS, ATTN_HEAD_DIM), k_g)
    v = qkv[..., nq + nk:].reshape(bsz, seq_len, ATTN_KV_HEADS, ATTN_HEAD_DIM)
    return q, k, v


def block_attention(q, k, v):
    bsz, lq = q.shape[:2]
    nb = lq // Q_BLOCK
    r = ATTN_HEADS // ATTN_KV_HEADS
    qb = jnp.moveaxis(q.reshape(bsz, nb, Q_BLOCK, ATTN_KV_HEADS, r, ATTN_HEAD_DIM), 1, 0)
    scale = ATTN_HEAD_DIM ** -0.5

    def one_block(qblk):
        s = jnp.einsum('bqgrd,bkgd->bgrqk', qblk, k).astype(jnp.float32) * scale
        p = jax.nn.softmax(s, axis=-1).astype(v.dtype)
        return jnp.einsum('bgrqk,bkgd->bqgrd', p, v)

    o = lax.map(one_block, qb)
    return jnp.moveaxis(o, 0, 1).reshape(bsz, lq, ATTN_HEADS * ATTN_HEAD_DIM)


def attn_context(h, w_qkv, q_g, k_g, w_o):
    q, k, v = attn_qkv(h, w_qkv, q_g, k_g)
    return block_attention(q, k, v) @ w_o, k, v


def attn_latent(h, ctx_k, ctx_v, w_qkv, q_g, k_g, w_o):
    q, k, v = attn_qkv(h, w_qkv, q_g, k_g)
    cos, sin = axial_rope_tables(h.shape[1])
    q = axial_rope(q, cos, sin)
    k = axial_rope(k, cos, sin)
    k_all = jnp.concatenate([ctx_k.astype(k.dtype), k], axis=1)
    v_all = jnp.concatenate([ctx_v.astype(v.dtype), v], axis=1)
    return block_attention(q, k_all, v_all) @ w_o


def conv_ffn(h, w_in, dw, dw_b, w_out):
    u = depthwise_conv(h @ w_in, dw, dw_b)
    return (jax.nn.silu(u[..., :FFN_DIM]) * u[..., FFN_DIM:]) @ w_out


def ffn_sublayer(x, mod, g_pre, g_post, w_in, dw, dw_b, w_out):
    h = modulate_in(x, mod, 1, g_pre)
    return gated_residual(x, conv_ffn(h, w_in, dw, dw_b, w_out), mod, 1, g_post)


def _nrm(k, shape, scale):
    return jax.random.normal(k, shape, jnp.float32) * scale


def setup_inputs(seed: int = 0) -> dict:
    key = jax.random.key(seed)
    k = jax.random.split(key, 36)
    D = D_MODEL
    nq = (ATTN_HEADS + 2 * ATTN_KV_HEADS) * ATTN_HEAD_DIM
    dt0 = jnp.exp(jax.random.uniform(k[15], (N_LAYERS_SSD, 2, SSD_HEADS), jnp.float32,
                                     minval=math.log(1e-3), maxval=math.log(1e-1)))
    return {
        'x_prompt': _nrm(k[0], (BATCH, SEQ, D), 1.0),
        'x_sample': _nrm(k[1], (DEC_BATCH, DEC_SEQ, D), 1.0),
        'state_ssd': _nrm(k[2], (DEC_BATCH, N_LAYERS_SSD, 2, SSD_HEADS, SSD_HEAD_DIM, SSD_STATE), 0.5),
        'cache_k': _nrm(k[3], (DEC_BATCH, N_LAYERS_ATTN, PAST_LEN, ATTN_KV_HEADS, ATTN_HEAD_DIM), 1.0),
        'cache_v': _nrm(k[4], (DEC_BATCH, N_LAYERS_ATTN, PAST_LEN, ATTN_KV_HEADS, ATTN_HEAD_DIM), 1.0),
        'c': _nrm(k[5], (DEC_BATCH, D), 1.0),
        'c_ctx': _nrm(k[6], (D,), 1.0),
        'w_mod': _nrm(k[7], (DEPTH, D, N_MOD * D), 0.5 * D ** -0.5),
        'b_mod': _nrm(k[8], (DEPTH, N_MOD * D), 0.02),
        'norm_g': 1.0 + _nrm(k[9], (DEPTH, 4, D), 0.1),
        'pool_w': _nrm(k[10], (N_LAYERS_POOL, len(POOL_WINDOWS), POOL_GROUP, POOL_GROUP), POOL_GROUP ** -0.5),
        'pool_scale': 1.0 + _nrm(k[11], (N_LAYERS_POOL, D), 0.1),
        'ssd_w_in': _nrm(k[12], (N_LAYERS_SSD, D, SSD_IN_DIM), D ** -0.5),
        'ssd_conv_w': _nrm(k[13], (N_LAYERS_SSD, SSD_CONV, SSD_CONV_DIM), SSD_CONV ** -0.5),
        'ssd_conv_b': _nrm(k[14], (N_LAYERS_SSD, SSD_CONV_DIM), 0.02),
        'ssd_dt_bias': dt0 + jnp.log(-jnp.expm1(-dt0)),
        'ssd_a_log': jnp.log(jax.random.uniform(k[16], (N_LAYERS_SSD, 2, SSD_HEADS), jnp.float32, minval=1.0, maxval=16.0)),
        'ssd_d': 1.0 + _nrm(k[17], (N_LAYERS_SSD, 2, SSD_HEADS), 0.1),
        'ssd_norm_g': 1.0 + _nrm(k[18], (N_LAYERS_SSD, SSD_D_INNER), 0.1),
        'ssd_w_out': _nrm(k[19], (N_LAYERS_SSD, SSD_D_INNER, D), SSD_D_INNER ** -0.5),
        'conf_w_in': _nrm(k[20], (N_LAYERS_CONV, D, 2 * D), D ** -0.5),
        'conf_b_in': _nrm(k[21], (N_LAYERS_CONV, 2 * D), 0.02),
        'conf_dw': _nrm(k[22], (N_LAYERS_CONV, CONF_KERNEL, D), CONF_KERNEL ** -0.5),
        'conf_dw_b': _nrm(k[23], (N_LAYERS_CONV, D), 0.02),
        'conf_ln_g': 1.0 + _nrm(k[24], (N_LAYERS_CONV, D), 0.1),
        'conf_ln_b': _nrm(k[25], (N_LAYERS_CONV, D), 0.02),
        'conf_w_out': _nrm(k[26], (N_LAYERS_CONV, D, D), D ** -0.5),
        'conf_b_out': _nrm(k[27], (N_LAYERS_CONV, D), 0.02),
        'attn_w_qkv': _nrm(k[28], (N_LAYERS_ATTN, D, nq), D ** -0.5),
        'attn_q_g': 1.0 + _nrm(k[29], (N_LAYERS_ATTN, ATTN_HEAD_DIM), 0.1),
        'attn_k_g': 1.0 + _nrm(k[30], (N_LAYERS_ATTN, ATTN_HEAD_DIM), 0.1),
        'attn_w_o': _nrm(k[31], (N_LAYERS_ATTN, ATTN_HEADS * ATTN_HEAD_DIM, D), (ATTN_HEADS * ATTN_HEAD_DIM) ** -0.5),
        'ffn_w_in': _nrm(k[32], (DEPTH, D, 2 * FFN_DIM), D ** -0.5),
        'ffn_dw': _nrm(k[33], (DEPTH, FFN_CONV, 2 * FFN_DIM), FFN_CONV ** -0.5),
        'ffn_dw_b': _nrm(k[34], (DEPTH, 2 * FFN_DIM), 0.02),
        'ffn_w_out': _nrm(k[35], (DEPTH, FFN_DIM, D), FFN_DIM ** -0.5),
    }


def reference(x_prompt, x_sample, state_ssd, cache_k, cache_v, c, c_ctx,
              w_mod, b_mod, norm_g, pool_w, pool_scale,
              ssd_w_in, ssd_conv_w, ssd_conv_b, ssd_dt_bias, ssd_a_log, ssd_d, ssd_norm_g, ssd_w_out,
              conf_w_in, conf_b_in, conf_dw, conf_dw_b, conf_ln_g, conf_ln_b, conf_w_out, conf_b_out,
              attn_w_qkv, attn_q_g, attn_k_g, attn_w_o,
              ffn_w_in, ffn_dw, ffn_dw_b, ffn_w_out):
    x = x_prompt
    ssd_states, ctx_ks, ctx_vs = [], [], []
    for i in range(DEPTH):
        kind, j = i % N_MIXERS, i // N_MIXERS
        mod = modulation(c_ctx[None], w_mod[i], b_mod[i])
        h = modulate_in(x, mod, 0, norm_g[i, 0])
        if kind == 0:
            out = pool_mixer(h, pool_w[j], pool_scale[j])
        elif kind == 1:
            h0 = jnp.zeros((h.shape[0], 2, SSD_HEADS, SSD_HEAD_DIM, SSD_STATE), jnp.float32)
            out, st = ssd_mixer(h, h0, ssd_w_in[j], ssd_conv_w[j], ssd_conv_b[j], ssd_dt_bias[j],
                                ssd_a_log[j], ssd_d[j], ssd_norm_g[j], ssd_w_out[j])
            ssd_states.append(st)
        elif kind == 2:
            out = conformer_conv(h, conf_w_in[j], conf_b_in[j], conf_dw[j], conf_dw_b[j],
                                 conf_ln_g[j], conf_ln_b[j], conf_w_out[j], conf_b_out[j])
        else:
            out, kc, vc = attn_context(h, attn_w_qkv[j], attn_q_g[j], attn_k_g[j], attn_w_o[j])
            ctx_ks.append(kc)
            ctx_vs.append(vc)
        x = gated_residual(x, out, mod, 0, norm_g[i, 1])
        x = ffn_sublayer(x, mod, norm_g[i, 2], norm_g[i, 3], ffn_w_in[i], ffn_dw[i], ffn_dw_b[i], ffn_w_out[i])
    y_prompt = x
    new_state_ssd = jnp.stack(ssd_states, axis=1)
    new_cache_k = jnp.stack(ctx_ks, axis=1)
    new_cache_v = jnp.stack(ctx_vs, axis=1)

    x = x_sample
    for i in range(DEPTH):
        kind, j = i % N_MIXERS, i // N_MIXERS
        mod = modulation(c, w_mod[i], b_mod[i])
        h = modulate_in(x, mod, 0, norm_g[i, 0])
        if kind == 0:
            out = pool_mixer(h, pool_w[j], pool_scale[j])
        elif kind == 1:
            out, _ = ssd_mixer(h, state_ssd[:, j], ssd_w_in[j], ssd_conv_w[j], ssd_conv_b[j], ssd_dt_bias[j],
                               ssd_a_log[j], ssd_d[j], ssd_norm_g[j], ssd_w_out[j])
        elif kind == 2:
            out = conformer_conv(h, conf_w_in[j], conf_b_in[j], conf_dw[j], conf_dw_b[j],
                                 conf_ln_g[j], conf_ln_b[j], conf_w_out[j], conf_b_out[j])
        else:
            out = attn_latent(h, cache_k[:, j], cache_v[:, j], attn_w_qkv[j], attn_q_g[j], attn_k_g[j], attn_w_o[j])
        x = gated_residual(x, out, mod, 0, norm_g[i, 1])
        x = ffn_sublayer(x, mod, norm_g[i, 2], norm_g[i, 3], ffn_w_in[i], ffn_dw[i], ffn_dw_b[i], ffn_w_out[i])
    y_sample = x
    return (y_prompt, y_sample, new_state_ssd, new_cache_k, new_cache_v)
```

```python
import functools
import math

import jax
import jax.numpy as jnp
from jax import lax
from jax.experimental import pallas as pl
from jax.experimental.pallas import tpu as pltpu

F32 = jnp.float32
BF16 = jnp.bfloat16

EPS = 1e-6
N_MOD = 6
MOD_ROWS = 8
POOL_WINDOWS = (2, 4, 8, 16)
SSD_HEAD_DIM = 64
SSD_GROUPS = 8
SSD_STATE = 128
SSD_CHUNK = 128
CONF_KERNEL = 31
ATTN_HEADS = 8
ATTN_KV_HEADS = 2
ATTN_HEAD_DIM = 128
ROPE_THETA = 10000.0
GRID_W = 64

SUBLANES = 8
LANES = 128
VMEM_LIMIT = 56 * 1024 * 1024


def _cparams(sem):
    return pltpu.CompilerParams(dimension_semantics=sem, vmem_limit_bytes=VMEM_LIMIT)


def _sigmoid(x):
    return 1.0 / (1.0 + jnp.exp(-x))


def _silu(x):
    return x * _sigmoid(x)


def _rms(x, g):
    ms = jnp.mean(x * x, axis=-1, keepdims=True)
    return (x * lax.rsqrt(ms + EPS)) * g


def _modin(x, g, shift, scale):
    return _rms(x, g) * (1.0 + scale) + shift


def _split3(v):
    hi = v.astype(BF16)
    r1 = v - hi.astype(F32)
    mid = r1.astype(BF16)
    lo = (r1 - mid.astype(F32)).astype(BF16)
    return hi, mid, lo


def _dot(a, b):
    return jnp.dot(a, b, preferred_element_type=F32)


def _dot_sel_r(v, sel):
    hi, mid, lo = _split3(v)
    return _dot(hi, sel) + _dot(mid, sel) + _dot(lo, sel)


def _dot_sel_l(sel, v):
    hi, mid, lo = _split3(v)
    return _dot(sel, hi) + _dot(sel, mid) + _dot(sel, lo)


class _Pass:
    def __init__(self, batch, seq, mod_base, per_batch):
        self.B, self.L, self.mod_base, self.per_batch = batch, seq, mod_base, per_batch
        self.T = batch * seq

    def modrow(self, m, tm):
        if self.per_batch:
            return self.mod_base + (m * tm) // self.L
        return self.mod_base


def _mod_kernel(c_ref, w_ref, b_ref, o_ref):
    s = _silu(c_ref[...]).astype(BF16)
    o_ref[0] = _dot(s, w_ref[0].astype(BF16)) + b_ref[0]


def _modulation(cond, w_mod, b_mod):
    depth, d, n = w_mod.shape
    tn = 1024
    return pl.pallas_call(
        _mod_kernel,
        grid=(depth, n // tn),
        in_specs=[
            pl.BlockSpec((MOD_ROWS, d), lambda i, j: (0, 0)),
            pl.BlockSpec((1, d, tn), lambda i, j: (i, 0, j)),
            pl.BlockSpec((1, 1, tn), lambda i, j: (i, 0, j)),
        ],
        out_specs=pl.BlockSpec((1, MOD_ROWS, tn), lambda i, j: (i, 0, j)),
        out_shape=jax.ShapeDtypeStruct((depth, MOD_ROWS, n), F32),
        compiler_params=_cparams(("parallel", "parallel")),
        name="modulation",
    )(cond, w_mod, b_mod.reshape(depth, 1, n))


def _min_mm_kernel(x_ref, mod_ref, g_ref, w_ref, b_ref, o_ref, h_s):
    @pl.when(pl.program_id(1) == 0)
    def _():
        mod = mod_ref[0, 0]
        h = _modin(x_ref[...], g_ref[0, 0:1], mod[0:1], mod[1:2])
        h_s[...] = h.astype(BF16)

    o_ref[...] = (_dot(h_s[...], w_ref[0]) + b_ref[0]).astype(o_ref.dtype)


def _modin_matmul(p, x, mods, norm_g, layer, w, b, wl, *, tm, tn, out_dtype=F32):
    t, d = x.shape
    n = w.shape[-1]
    return pl.pallas_call(
        _min_mm_kernel,
        grid=(t // tm, n // tn),
        in_specs=[
            pl.BlockSpec((tm, d), lambda m, j: (m, 0)),
            pl.BlockSpec((1, 1, N_MOD, d), lambda m, j: (layer, p.modrow(m, tm), 0, 0)),
            pl.BlockSpec((1, 4, d), lambda m, j: (layer, 0, 0)),
            pl.BlockSpec((1, d, tn), lambda m, j: (wl, 0, j)),
            pl.BlockSpec((1, 1, tn), lambda m, j: (wl, 0, j)),
        ],
        out_specs=pl.BlockSpec((tm, tn), lambda m, j: (m, j)),
        out_shape=jax.ShapeDtypeStruct((t, n), out_dtype),
        scratch_shapes=[pltpu.VMEM((tm, d), BF16)],
        compiler_params=_cparams(("parallel", "arbitrary")),
        name="modin_matmul",
    )(x, mods, norm_g, w, b)


def _gated(x, out, gate, g_post):
    return x + gate * _rms(out, g_post)


def _mm_res_kernel(u_ref, x_ref, mod_ref, g_ref, w_ref, o_ref):
    mod = mod_ref[0, 0]
    out = _dot(u_ref[...].astype(BF16), w_ref[0])
    o_ref[...] = _gated(x_ref[...], out, mod[2:3], g_ref[0, 1:2])


def _matmul_resid(p, u, x, mods, norm_g, layer, w, wl, *, tm):
    t, d = x.shape
    k = u.shape[-1]
    return pl.pallas_call(
        _mm_res_kernel,
        grid=(t // tm,),
        in_specs=[
            pl.BlockSpec((tm, k), lambda m: (m, 0)),
            pl.BlockSpec((tm, d), lambda m: (m, 0)),
            pl.BlockSpec((1, 1, N_MOD, d), lambda m: (layer, p.modrow(m, tm), 0, 0)),
            pl.BlockSpec((1, 4, d), lambda m: (layer, 0, 0)),
            pl.BlockSpec((1, k, d), lambda m: (wl, 0, 0)),
        ],
        out_specs=pl.BlockSpec((tm, d), lambda m: (m, 0)),
        out_shape=jax.ShapeDtypeStruct((t, d), F32),
        compiler_params=_cparams(("parallel",)),
        name="matmul_resid",
    )(u, x, mods, norm_g, w)


def _seg_layout(L, tm, pad):
    S = min(L, tm)
    nseg = tm // S
    mext = nseg * (S + pad) + pad
    halo = L > tm
    return S, nseg, mext, halo


def _halo_specs(t, tm, width, pad):
    r = tm // pad
    last = t // pad - 1
    prev = pl.BlockSpec((pad, width), lambda m, *_: (jnp.maximum(m * r - 1, 0), 0))
    nxt = pl.BlockSpec((pad, width), lambda m, *_: (jnp.minimum((m + 1) * r, last), 0))
    return prev, nxt


def _ffn_kernel(*refs, L, tm, nj):
    S, nseg, mext, halo = _seg_layout(L, tm, SUBLANES)
    if halo:
        x_ref, xp_ref, xn_ref = refs[:3]
        refs = refs[3:]
    else:
        x_ref = refs[0]
        refs = refs[1:]
    mod_ref, g_ref, w1_ref, w2_ref, dw1_ref, dw2_ref, b1_ref, b2_ref, wo_ref, o_ref, h_s, acc_s = refs
    m = pl.program_id(0)
    j = pl.program_id(1)
    d = x_ref.shape[-1]
    mod = mod_ref[0, 0]
    shift, scale, gate = mod[3:4], mod[4:5], mod[5:6]
    g_pre, g_post = g_ref[0, 2:3], g_ref[0, 3:4]
    stride = S + SUBLANES

    @pl.when(j == 0)
    def _():
        zpad = jnp.zeros((SUBLANES, d), BF16)
        for s in range(nseg):
            base = s * stride
            h_s[base:base + SUBLANES, :] = zpad
            h_s[base + SUBLANES:base + stride, :] = _modin(
                x_ref[s * S:(s + 1) * S, :], g_pre, shift, scale).astype(BF16)
        h_s[nseg * stride:mext, :] = zpad
        if halo:
            pos = (m * tm) % L
            hp = _modin(xp_ref[...], g_pre, shift, scale)
            hn = _modin(xn_ref[...], g_pre, shift, scale)
            h_s[0:SUBLANES, :] = jnp.where(pos == 0, 0.0, hp).astype(BF16)
            h_s[stride:mext, :] = jnp.where(pos + tm == L, 0.0, hn).astype(BF16)
        acc_s[...] = jnp.zeros_like(acc_s)

    hs = h_s[...]

    def conv(u, dw_ref, b_ref):
        w = dw_ref[0]
        prev = pltpu.roll(u, 1, 0)
        nxt = pltpu.roll(u, mext - 1, 0)
        return prev * w[0:1] + u * w[1:2] + nxt * w[2:3] + b_ref[0]

    c1 = conv(_dot(hs, w1_ref[0]), dw1_ref, b1_ref)
    c2 = conv(_dot(hs, w2_ref[0]), dw2_ref, b2_ref)
    a = (_silu(c1) * c2).astype(BF16)
    if nseg > 1:
        a = jnp.concatenate([a[s * stride + SUBLANES:(s + 1) * stride] for s in range(nseg)], axis=0)
    else:
        a = a[SUBLANES:stride]
    acc_s[...] += _dot(a, wo_ref[0])

    @pl.when(j == nj - 1)
    def _():
        o_ref[...] = _gated(x_ref[...], acc_s[...], gate, g_post)


def _ffn(p, x, mods, norm_g, layer, w_in, dw, dw_b, w_out, *, tm, tf):
    t, d = x.shape
    f = w_out.shape[1]
    nj = f // tf
    S, nseg, mext, halo = _seg_layout(p.L, tm, SUBLANES)
    in_specs = [pl.BlockSpec((tm, d), lambda m, j: (m, 0))]
    args = [x]
    if halo:
        in_specs += list(_halo_specs(t, tm, d, SUBLANES))
        args += [x, x]
    in_specs += [
        pl.BlockSpec((1, 1, N_MOD, d), lambda m, j: (layer, p.modrow(m, tm), 0, 0)),
        pl.BlockSpec((1, 4, d), lambda m, j: (layer, 0, 0)),
        pl.BlockSpec((1, d, tf), lambda m, j: (layer, 0, j)),
        pl.BlockSpec((1, d, tf), lambda m, j: (layer, 0, j + nj)),
        pl.BlockSpec((1, 3, tf), lambda m, j: (layer, 0, j)),
        pl.BlockSpec((1, 3, tf), lambda m, j: (layer, 0, j + nj)),
        pl.BlockSpec((1, 1, tf), lambda m, j: (layer, 0, j)),
        pl.BlockSpec((1, 1, tf), lambda m, j: (layer, 0, j + nj)),
        pl.BlockSpec((1, tf, d), lambda m, j: (layer, j, 0)),
    ]
    args += [mods, norm_g, w_in, w_in, dw, dw, dw_b, dw_b, w_out]
    return pl.pallas_call(
        functools.partial(_ffn_kernel, L=p.L, tm=tm, nj=nj),
        grid=(t // tm, nj),
        in_specs=in_specs,
        out_specs=pl.BlockSpec((tm, d), lambda m, j: (m, 0)),
        out_shape=jax.ShapeDtypeStruct((t, d), F32),
        scratch_shapes=[pltpu.VMEM((mext, d), BF16), pltpu.VMEM((tm, d), F32)],
        compiler_params=_cparams(("parallel", "arbitrary")),
        name="conv_ffn",
    )(*args)


def _pool_kernel(*refs, L, tm):
    S, nseg, mext, halo = _seg_layout(L, tm, SUBLANES)
    if halo:
        x_ref, xp_ref, xn_ref = refs[:3]
        refs = refs[3:]
    else:
        x_ref = refs[0]
        refs = refs[1:]
    mod_ref, g_ref, w_ref, sc_ref, o_ref, h_s, y_s = refs
    m = pl.program_id(0)
    d = x_ref.shape[-1]
    mod = mod_ref[0, 0]
    shift, scale, gate = mod[0:1], mod[1:2], mod[2:3]
    g_pre, g_post = g_ref[0, 0:1], g_ref[0, 1:2]
    stride = S + SUBLANES
    grp = d // len(POOL_WINDOWS)

    zpad = jnp.zeros((SUBLANES, d), F32)
    for s in range(nseg):
        base = s * stride
        h_s[base:base + SUBLANES, :] = zpad
        h_s[base + SUBLANES:base + stride, :] = _modin(x_ref[s * S:(s + 1) * S, :], g_pre, shift, scale)
    h_s[nseg * stride:mext, :] = zpad
    pos0 = (m * tm) % L if halo else 0
    if halo:
        hp = _modin(xp_ref[...], g_pre, shift, scale)
        hn = _modin(xn_ref[...], g_pre, shift, scale)
        h_s[0:SUBLANES, :] = jnp.where(pos0 == 0, 0.0, hp)
        h_s[stride:mext, :] = jnp.where(pos0 + tm == L, 0.0, hn)

    t_in_seq = lax.broadcasted_iota(jnp.int32, (S, grp), 0) + pos0
    for s in range(nseg):
        row0 = s * stride + SUBLANES
        for gi, win in enumerate(POOL_WINDOWS):
            cols = slice(gi * grp, (gi + 1) * grp)
            tot = h_s[row0 - win // 2:row0 - win // 2 + S, cols]
            for k in range(1, win):
                off = row0 - win // 2 + k
                tot = tot + h_s[off:off + S, cols]
            lo = jnp.maximum(t_in_seq - win // 2, 0)
            hi = jnp.minimum(t_in_seq + (win - win // 2), L)
            cnt = (hi - lo).astype(F32)
            pooled = tot / cnt - h_s[row0:row0 + S, cols]
            y = _dot(pooled.astype(BF16), w_ref[0, gi])
            y_s[s * S:(s + 1) * S, cols] = y * sc_ref[0, :, cols]
    o_ref[...] = _gated(x_ref[...], y_s[...], gate, g_post)


def _pool_sublayer(p, x, mods, norm_g, layer, pool_w, pool_scale, wl, *, tm):
    t, d = x.shape
    S, nseg, mext, halo = _seg_layout(p.L, tm, SUBLANES)
    nwin, grp, _ = pool_w.shape[1:]
    in_specs = [pl.BlockSpec((tm, d), lambda m: (m, 0))]
    args = [x]
    if halo:
        in_specs += list(_halo_specs(t, tm, d, SUBLANES))
        args += [x, x]
    in_specs += [
        pl.BlockSpec((1, 1, N_MOD, d), lambda m: (layer, p.modrow(m, tm), 0, 0)),
        pl.BlockSpec((1, 4, d), lambda m: (layer, 0, 0)),
        pl.BlockSpec((1, nwin, grp, grp), lambda m: (wl, 0, 0, 0)),
        pl.BlockSpec((1, 1, d), lambda m: (wl, 0, 0)),
    ]
    args += [mods, norm_g, pool_w, pool_scale]
    return pl.pallas_call(
        functools.partial(_pool_kernel, L=p.L, tm=tm),
        grid=(t // tm,),
        in_specs=in_specs,
        out_specs=pl.BlockSpec((tm, d), lambda m: (m, 0)),
        out_shape=jax.ShapeDtypeStruct((t, d), F32),
        scratch_shapes=[pltpu.VMEM((mext, d), F32), pltpu.VMEM((tm, d), F32)],
        compiler_params=_cparams(("parallel",)),
        name="pool_sublayer",
    )(*args)


def _ssd_core_kernel(*refs, L, has_h0, want_state):
    refs = list(refs)
    xg_ref, bg_ref, cg_ref, dt_ref = refs[:4]
    cwx_ref, cwb_ref, cwc_ref, cbx_ref, cbb_ref, cbc_ref = refs[4:10]
    dtb_ref, alog_ref, skip_ref = refs[10:13]
    refs = refs[13:]
    if has_h0:
        h0_ref = refs[0]
        refs = refs[1:]
    y_ref = refs[0]
    refs = refs[1:]
    if want_state:
        st_ref = refs[0]
        refs = refs[1:]
    xs, bs, cs, dt8s, dta8s, ht_s = refs
    q = SSD_CHUNK
    nc = L // q
    p_dim = SSD_HEAD_DIM
    r_heads = xs.shape[-1] // p_dim
    n_heads = SSD_GROUPS * r_heads
    g = pl.program_id(1)

    def conv_silu(ref, w_ref, b_ref, out):
        v = ref[0]
        rows = lax.broadcasted_iota(jnp.int32, v.shape, 0)
        prev = jnp.where(rows == 0, 0.0, pltpu.roll(v, 1, 0))
        nxt = jnp.where(rows == L - 1, 0.0, pltpu.roll(v, L - 1, 0))
        w = w_ref[0]
        out[...] = _silu(prev * w[0:1] + v * w[1:2] + nxt * w[2:3] + b_ref[0])

    conv_silu(xg_ref, cwx_ref, cbx_ref, xs)
    conv_silu(bg_ref, cwb_ref, cbb_ref, bs)
    conv_silu(cg_ref, cwc_ref, cbc_ref, cs)

    raw = dt_ref[0] + dtb_ref[0]
    dt_all = jnp.maximum(raw, 0.0) + jnp.log(1.0 + jnp.exp(-jnp.abs(raw)))
    dta_all = dt_all * (-jnp.exp(alog_ref[0]))
    kk = lax.broadcasted_iota(jnp.int32, (LANES, LANES), 0)
    cc = lax.broadcasted_iota(jnp.int32, (LANES, LANES), 1)
    src = jnp.where(cc < r_heads, g * r_heads + cc, n_heads + g * r_heads + cc - r_heads)
    sel = jnp.where((cc < 2 * r_heads) & (kk == src), 1.0, 0.0).astype(BF16)
    dt8s[...] = _dot_sel_r(dt_all, sel)
    dta8s[...] = _dot_sel_r(dta_all, sel)

    ii = lax.broadcasted_iota(jnp.int32, (q, q), 0)
    jj = lax.broadcasted_iota(jnp.int32, (q, q), 1)
    ek = lax.broadcasted_iota(jnp.int32, (LANES, r_heads * p_dim), 0)
    ec = lax.broadcasted_iota(jnp.int32, (LANES, r_heads * p_dim), 1) // p_dim

    def run_direction(dirn):
        causal = (jj <= ii) if dirn == 0 else (jj >= ii)
        tri = jnp.where(causal, 1.0, 0.0).astype(BF16)
        expand = jnp.where(ek == ec + dirn * r_heads, 1.0, 0.0).astype(BF16)
        edge = q - 1 if dirn == 0 else 0
        if has_h0:
            h0 = h0_ref[0, 0, dirn].reshape(r_heads * p_dim, SSD_STATE)
            ht_s[...] = h0.T
        else:
            ht_s[...] = jnp.zeros_like(ht_s)

        def body(ci, carry):
            c = ci if dirn == 0 else nc - 1 - ci
            rows = pl.ds(pl.multiple_of(c * q, q), q)
            xq, bq, cq = xs[rows, :], bs[rows, :], cs[rows, :]
            cum = _dot_sel_l(tri, dta8s[rows, :])
            cum_t = cum.T
            cb = lax.dot_general(cq.astype(BF16), bq.astype(BF16), (((1,), (1,)), ((), ())),
                                 preferred_element_type=F32)
            dt_x = _dot_sel_r(dt8s[rows, :], expand)
            cum_x = _dot_sel_r(cum, expand)
            last_x = cum_x[edge:edge + 1, :]
            xdt = xq * dt_x
            xdt_b = xdt.astype(BF16)
            parts = []
            for r in range(r_heads):
                k = dirn * r_heads + r
                seg = cum[:, k:k + 1] - cum_t[k:k + 1, :]
                wm = (cb * jnp.exp(jnp.where(causal, seg, -jnp.inf))).astype(BF16)
                parts.append(_dot(wm, xdt_b[:, r * p_dim:(r + 1) * p_dim]))
            y_diag = jnp.concatenate(parts, axis=1)
            ht = ht_s[...]
            y_off = _dot(cq.astype(BF16), ht.astype(BF16)) * jnp.exp(cum_x)
            xsc = (xdt * jnp.exp(last_x - cum_x)).astype(BF16)
            s_t = _dot(bq.T.astype(BF16), xsc)
            ht_s[...] = ht * jnp.exp(last_x) + s_t
            if dirn == 0:
                y_ref[0, rows, :] = y_diag + y_off + xq * skip_ref[0]
            else:
                y_ref[0, rows, :] = y_ref[0, rows, :] + y_diag + y_off
            return carry

        lax.fori_loop(0, nc, body, 0)
        if want_state:
            st_ref[0, 0, dirn] = ht_s[...].T.reshape(r_heads, p_dim, SSD_STATE)

    run_direction(0)
    run_direction(1)


def _ssd_core(p, proj, conv_w, conv_b, dt_bias, a_log, skip_x, wl, h0, want_state):
    bsz, seq = p.B, p.L
    gn = SSD_GROUPS * SSD_STATE
    d_inner = conv_w.shape[-1] - 2 * gn
    xw = d_inner // SSD_GROUPS
    r_heads = xw // SSD_HEAD_DIM
    n_heads = SSD_GROUPS * r_heads
    xb0 = d_inner // xw
    bb0 = 2 * d_inner // SSD_STATE
    cb0 = bb0 + SSD_GROUPS
    db0 = cb0 + SSD_GROUPS
    in_specs = [
        pl.BlockSpec((1, seq, xw), lambda b, g: (b, 0, xb0 + g)),
        pl.BlockSpec((1, seq, SSD_STATE), lambda b, g: (b, 0, bb0 + g)),
        pl.BlockSpec((1, seq, SSD_STATE), lambda b, g: (b, 0, cb0 + g)),
        pl.BlockSpec((1, seq, LANES), lambda b, g: (b, 0, db0)),
        pl.BlockSpec((1, 3, xw), lambda b, g: (wl, 0, g)),
        pl.BlockSpec((1, 3, SSD_STATE), lambda b, g: (wl, 0, d_inner // SSD_STATE + g)),
        pl.BlockSpec((1, 3, SSD_STATE), lambda b, g: (wl, 0, d_inner // SSD_STATE + SSD_GROUPS + g)),
        pl.BlockSpec((1, 1, xw), lambda b, g: (wl, 0, g)),
        pl.BlockSpec((1, 1, SSD_STATE), lambda b, g: (wl, 0, d_inner // SSD_STATE + g)),
        pl.BlockSpec((1, 1, SSD_STATE), lambda b, g: (wl, 0, d_inner // SSD_STATE + SSD_GROUPS + g)),
        pl.BlockSpec((1, 1, LANES), lambda b, g: (wl, 0, 0)),
        pl.BlockSpec((1, 1, LANES), lambda b, g: (wl, 0, 0)),
        pl.BlockSpec((1, 1, xw), lambda b, g: (wl, 0, g)),
    ]
    args = [proj, proj, proj, proj, conv_w, conv_w, conv_w, conv_b, conv_b, conv_b, dt_bias, a_log, skip_x]
    if h0 is not None:
        in_specs.append(pl.BlockSpec((1, 1, 2, r_heads, SSD_HEAD_DIM, SSD_STATE), lambda b, g: (b, wl, 0, g, 0, 0)))
        args.append(h0)
    out_specs = [pl.BlockSpec((1, seq, xw), lambda b, g: (b, 0, g))]
    out_shape = [jax.ShapeDtypeStruct((bsz, seq, d_inner), F32)]
    if want_state:
        out_specs.append(pl.BlockSpec((1, 1, 2, r_heads, SSD_HEAD_DIM, SSD_STATE), lambda b, g: (b, 0, 0, g, 0, 0)))
        out_shape.append(jax.ShapeDtypeStruct((bsz, 1, 2, n_heads, SSD_HEAD_DIM, SSD_STATE), F32))
    res = pl.pallas_call(
        functools.partial(_ssd_core_kernel, L=seq, has_h0=h0 is not None, want_state=want_state),
        grid=(bsz, SSD_GROUPS),
        in_specs=in_specs,
        out_specs=out_specs,
        out_shape=out_shape,
        scratch_shapes=[
            pltpu.VMEM((seq, xw), F32), pltpu.VMEM((seq, SSD_STATE), F32), pltpu.VMEM((seq, SSD_STATE), F32),
            pltpu.VMEM((seq, LANES), F32), pltpu.VMEM((seq, LANES), F32),
            pltpu.VMEM((SSD_STATE, xw), F32),
        ],
        compiler_params=_cparams(("parallel", "arbitrary")),
        name="ssd_core",
    )(*args)
    return (res[0], res[1]) if want_state else (res[0], None)


def _ssd_out_kernel(y_ref, z_ref, x_ref, mod_ref, g_ref, ng_ref, w_ref, o_ref):
    mod = mod_ref[0, 0]
    u = _rms(y_ref[...] * _silu(z_ref[...]), ng_ref[0])
    out = _dot(u.astype(BF16), w_ref[0])
    o_ref[...] = _gated(x_ref[...], out, mod[2:3], g_ref[0, 1:2])


def _ssd_out(p, y, proj, x, mods, norm_g, layer, ssd_norm_g, w_out, wl, *, tm):
    t, d = x.shape
    k = y.shape[-1]
    return pl.pallas_call(
        _ssd_out_kernel,
        grid=(t // tm,),
        in_specs=[
            pl.BlockSpec((tm, k), lambda m: (m, 0)),
            pl.BlockSpec((tm, k), lambda m: (m, 0)),
            pl.BlockSpec((tm, d), lambda m: (m, 0)),
            pl.BlockSpec((1, 1, N_MOD, d), lambda m: (layer, p.modrow(m, tm), 0, 0)),
            pl.BlockSpec((1, 4, d), lambda m: (layer, 0, 0)),
            pl.BlockSpec((1, 1, k), lambda m: (wl, 0, 0)),
            pl.BlockSpec((1, k, d), lambda m: (wl, 0, 0)),
        ],
        out_specs=pl.BlockSpec((tm, d), lambda m: (m, 0)),
        out_shape=jax.ShapeDtypeStruct((t, d), F32),
        compiler_params=_cparams(("parallel",)),
        name="ssd_out",
    )(y, proj, x, mods, norm_g, ssd_norm_g, w_out)


def _conf_kernel(*refs, L, tm, pad):
    S, nseg, mext, halo = _seg_layout(L, tm, pad)
    if halo:
        u_ref, up_ref, un_ref = refs[:3]
        refs = refs[3:]
    else:
        u_ref = refs[0]
        refs = refs[1:]
    x_ref, mod_ref, g_ref, dw_ref, dwb_ref, lg_ref, lb_ref, w_ref, b_ref, o_ref, g_s, c_s = refs
    m = pl.program_id(0)
    d = x_ref.shape[-1]
    mod = mod_ref[0, 0]
    stride = S + pad
    left = (CONF_KERNEL - 1) // 2

    def glu(u):
        return u[:, :d] * _sigmoid(u[:, d:])

    zpad = jnp.zeros((pad, d), F32)
    for s in range(nseg):
        base = s * stride
        g_s[base:base + pad, :] = zpad
        g_s[base + pad:base + stride, :] = glu(u_ref[s * S:(s + 1) * S, :])
    g_s[nseg * stride:mext, :] = zpad
    if halo:
        pos0 = (m * tm) % L
        g_s[0:pad, :] = jnp.where(pos0 == 0, 0.0, glu(up_ref[...]))
        g_s[stride:mext, :] = jnp.where(pos0 + tm == L, 0.0, glu(un_ref[...]))

    dw = dw_ref[0]
    for s in range(nseg):
        row0 = s * stride + pad - left
        acc = g_s[row0:row0 + S, :] * dw[0:1]
        for k in range(1, CONF_KERNEL):
            acc = acc + g_s[row0 + k:row0 + k + S, :] * dw[k:k + 1]
        c_s[s * S:(s + 1) * S, :] = acc
    c = c_s[...] + dwb_ref[0]
    mu = jnp.mean(c, axis=-1, keepdims=True)
    cen = c - mu
    var = jnp.mean(cen * cen, axis=-1, keepdims=True)
    v = _silu(cen * lax.rsqrt(var + EPS) * lg_ref[0] + lb_ref[0])
    out = _dot(v.astype(BF16), w_ref[0]) + b_ref[0]
    o_ref[...] = _gated(x_ref[...], out, mod[2:3], g_ref[0, 1:2])


def _conf_sublayer(p, u, x, mods, norm_g, layer, dw, dw_b, ln_g, ln_b, w_out, b_out, wl, *, tm):
    t, d = x.shape
    pad = 2 * SUBLANES
    S, nseg, mext, halo = _seg_layout(p.L, tm, pad)
    in_specs = [pl.BlockSpec((tm, 2 * d), lambda m: (m, 0))]
    args = [u]
    if halo:
        in_specs += list(_halo_specs(t, tm, 2 * d, pad))
        args += [u, u]
    in_specs += [
        pl.BlockSpec((tm, d), lambda m: (m, 0)),
        pl.BlockSpec((1, 1, N_MOD, d), lambda m: (layer, p.modrow(m, tm), 0, 0)),
        pl.BlockSpec((1, 4, d), lambda m: (layer, 0, 0)),
        pl.BlockSpec((1, CONF_KERNEL, d), lambda m: (wl, 0, 0)),
        pl.BlockSpec((1, 1, d), lambda m: (wl, 0, 0)),
        pl.BlockSpec((1, 1, d), lambda m: (wl, 0, 0)),
        pl.BlockSpec((1, 1, d), lambda m: (wl, 0, 0)),
        pl.BlockSpec((1, d, d), lambda m: (wl, 0, 0)),
        pl.BlockSpec((1, 1, d), lambda m: (wl, 0, 0)),
    ]
    args += [x, mods, norm_g, dw, dw_b, ln_g, ln_b, w_out, b_out]
    return pl.pallas_call(
        functools.partial(_conf_kernel, L=p.L, tm=tm, pad=pad),
        grid=(t // tm,),
        in_specs=in_specs,
        out_specs=pl.BlockSpec((tm, d), lambda m: (m, 0)),
        out_shape=jax.ShapeDtypeStruct((t, d), F32),
        scratch_shapes=[pltpu.VMEM((mext, d), F32), pltpu.VMEM((tm, d), F32)],
        compiler_params=_cparams(("parallel",)),
        name="conformer_conv",
    )(*args)


def _rope_tables(seq):
    rows = seq // GRID_W
    row = jnp.repeat(jnp.arange(rows), GRID_W).astype(F32)
    col = jnp.tile(jnp.arange(GRID_W), rows).astype(F32)
    n_freq = ATTN_HEAD_DIM // 4
    inv = ROPE_THETA ** (-jnp.arange(n_freq, dtype=F32) / n_freq)
    ang = jnp.stack([row[:, None] * inv, col[:, None] * inv], axis=1)
    cos, sin = jnp.cos(ang), jnp.sin(ang)
    cos_t = jnp.concatenate([cos, cos], axis=-1).reshape(seq, ATTN_HEAD_DIM)
    sin_t = jnp.concatenate([-sin, sin], axis=-1).reshape(seq, ATTN_HEAD_DIM)
    return cos_t, sin_t


def _qk_prep_kernel(*refs, rope):
    if rope:
        qkv_ref, qg_ref, kg_ref, cos_ref, sin_ref, q_ref, k_ref, v_ref = refs
    else:
        qkv_ref, qg_ref, kg_ref, q_ref, k_ref, v_ref = refs
    hd = ATTN_HEAD_DIM
    nq, nk = ATTN_HEADS * hd, ATTN_KV_HEADS * hd
    n_freq = hd // 4
    if rope:
        lane = lax.broadcasted_iota(jnp.int32, (qkv_ref.shape[0], hd), 1)
        first_half = (lane % (2 * n_freq)) < n_freq
        cos_t, sin_t = cos_ref[...], sin_ref[...]

    def head(col, g):
        xh = qkv_ref[:, col:col + hd]
        y = _rms(xh, g)
        if rope:
            partner = jnp.where(first_half, pltpu.roll(y, hd - n_freq, 1), pltpu.roll(y, n_freq, 1))
            y = y * cos_t + partner * sin_t
        return y

    qscale = hd ** -0.5
    for h in range(ATTN_HEADS):
        q_ref[:, h * hd:(h + 1) * hd] = (head(h * hd, qg_ref[0]) * qscale).astype(q_ref.dtype)
    for h in range(ATTN_KV_HEADS):
        k_ref[:, h * hd:(h + 1) * hd] = head(nq + h * hd, kg_ref[0])
    v_ref[...] = qkv_ref[:, nq + nk:]


def _qk_prep(p, qkv, q_g, k_g, wl, rope, *, tm):
    t = qkv.shape[0]
    hd = ATTN_HEAD_DIM
    nq, nk = ATTN_HEADS * hd, ATTN_KV_HEADS * hd
    in_specs = [
        pl.BlockSpec((tm, nq + 2 * nk), lambda m: (m, 0)),
        pl.BlockSpec((1, 1, hd), lambda m: (wl, 0, 0)),
        pl.BlockSpec((1, 1, hd), lambda m: (wl, 0, 0)),
    ]
    args = [qkv, q_g, k_g]
    if rope:
        cos_t, sin_t = _rope_tables(p.L)
        nb = p.L // tm
        in_specs += [pl.BlockSpec((tm, hd), lambda m: (m % nb, 0))] * 2
        args += [cos_t, sin_t]
    return pl.pallas_call(
        functools.partial(_qk_prep_kernel, rope=rope),
        grid=(t // tm,),
        in_specs=in_specs,
        out_specs=[
            pl.BlockSpec((tm, nq), lambda m: (m, 0)),
            pl.BlockSpec((tm, nk), lambda m: (m, 0)),
            pl.BlockSpec((tm, nk), lambda m: (m, 0)),
        ],
        out_shape=[
            jax.ShapeDtypeStruct((t, nq), BF16),
            jax.ShapeDtypeStruct((t, nk), F32),
            jax.ShapeDtypeStruct((t, nk), F32),
        ],
        compiler_params=_cparams(("parallel",)),
        name="qk_prep",
    )(*args)


def _attn_kernel(*refs, cached):
    if cached:
        q_ref, k_ref, v_ref, kc_ref, vc_ref, o_ref = refs
    else:
        q_ref, k_ref, v_ref, o_ref = refs
    hd = ATTN_HEAD_DIM
    rep = ATTN_HEADS // ATTN_KV_HEADS
    nt = (((1,), (1,)), ((), ()))
    for g in range(ATTN_KV_HEADS):
        cols = slice(g * hd, (g + 1) * hd)
        kg = k_ref[0, :, cols].astype(BF16)
        vg = v_ref[0, :, cols].astype(BF16)
        if cached:
            kcg = kc_ref[0, :, cols].astype(BF16)
            vcg = vc_ref[0, :, cols].astype(BF16)
        for r in range(rep):
            h = g * rep + r
            qh = q_ref[0, :, h * hd:(h + 1) * hd]
            s = lax.dot_general(qh, kg, nt, preferred_element_type=F32)
            mx = jnp.max(s, axis=-1, keepdims=True)
            if cached:
                sc = lax.dot_general(qh, kcg, nt, preferred_element_type=F32)
                mx = jnp.maximum(mx, jnp.max(sc, axis=-1, keepdims=True))
                pc = jnp.exp(sc - mx)
            pw = jnp.exp(s - mx)
            den = jnp.sum(pw, axis=-1, keepdims=True)
            num = _dot(pw.astype(BF16), vg)
            if cached:
                den = den + jnp.sum(pc, axis=-1, keepdims=True)
                num = num + _dot(pc.astype(BF16), vcg)
            o_ref[0, :, h * hd:(h + 1) * hd] = (num / den).astype(o_ref.dtype)


def _attention(p, q, k, v, kc, vc, cl, past, *, tq):
    bsz, seq, nq = q.shape
    nk = k.shape[-1]
    in_specs = [
        pl.BlockSpec((1, tq, nq), lambda b, i: (b, i, 0)),
        pl.BlockSpec((1, seq, nk), lambda b, i: (b, 0, 0)),
        pl.BlockSpec((1, seq, nk), lambda b, i: (b, 0, 0)),
    ]
    args = [q, k, v]
    if kc is not None:
        in_specs +=[pl.BlockSpec((1, past, nk), lambda b, i: (b, cl, 0))] * 2
        args += [kc, vc]
    return pl.pallas_call(
        functools.partial(_attn_kernel, cached=kc is not None),
        grid=(bsz, seq // tq),
        in_specs=in_specs,
        out_specs=pl.BlockSpec((1, tq, nq), lambda b, i: (b, i, 0)),
        out_shape=jax.ShapeDtypeStruct((bsz, seq, nq), BF16),
        compiler_params=_cparams(("parallel", "arbitrary")),
        name="attention",
    )(*args)


def _pad_last(a, n):
    return jnp.pad(a, [(0, 0)] * (a.ndim - 1) + [(0, n - a.shape[-1])])


def kernel(x_prompt, x_sample, state_ssd, cache_k, cache_v, c, c_ctx, w_mod, b_mod, norm_g, pool_w, pool_scale, ssd_w_in, ssd_conv_w, ssd_conv_b, ssd_dt_bias, ssd_a_log, ssd_d, ssd_norm_g, ssd_w_out, conf_w_in, conf_b_in, conf_dw, conf_dw_b, conf_ln_g, conf_ln_b, conf_w_out, conf_b_out, attn_w_qkv, attn_q_g, attn_k_g, attn_w_o, ffn_w_in, ffn_dw, ffn_dw_b, ffn_w_out):
    depth = w_mod.shape[0]
    d = x_prompt.shape[-1]
    dec_batch = x_sample.shape[0]
    passes = (
        _Pass(x_prompt.shape[0], x_prompt.shape[1], 0, False),
        _Pass(dec_batch, x_sample.shape[1], 1, True),
    )
    assert 1 + dec_batch <= MOD_ROWS

    cond = jnp.concatenate([c_ctx[None], c, jnp.zeros((MOD_ROWS - 1 - dec_batch, d), F32)], axis=0)
    mods = _modulation(cond, w_mod, b_mod).reshape(depth, MOD_ROWS, N_MOD, d)

    nl_ssd = ssd_w_in.shape[0]
    d_inner = ssd_w_out.shape[1]
    n_heads = ssd_dt_bias.shape[-1]
    ssd_main = ssd_w_in.shape[-1] - 2 * n_heads
    ssd_w_in_b = _pad_last(ssd_w_in, ssd_main + LANES).astype(BF16)
    ssd_zero_b = jnp.zeros((nl_ssd, 1, ssd_main + LANES), F32)
    ssd_dtb = _pad_last(ssd_dt_bias.reshape(nl_ssd, 1, 2 * n_heads), LANES)
    ssd_alog = _pad_last(ssd_a_log.reshape(nl_ssd, 1, 2 * n_heads), LANES)
    ssd_skip = jnp.repeat(jnp.sum(ssd_d, axis=1), SSD_HEAD_DIM, axis=-1).reshape(nl_ssd, 1, d_inner)
    ssd_cb = ssd_conv_b.reshape(nl_ssd, 1, -1)
    ssd_ng = ssd_norm_g.reshape(nl_ssd, 1, d_inner)
    ssd_w_out_b = ssd_w_out.astype(BF16)

    nl_conf = conf_w_in.shape[0]
    conf_w_in_b = conf_w_in.astype(BF16)
    conf_b_in_r = conf_b_in.reshape(nl_conf, 1, -1)
    conf_w_out_b = conf_w_out.astype(BF16)
    conf_vec = lambda a: a.reshape(nl_conf, 1, d)

    nl_attn = attn_w_qkv.shape[0]
    attn_w_qkv_b = attn_w_qkv.astype(BF16)
    attn_zero_b = jnp.zeros((nl_attn, 1, attn_w_qkv.shape[-1]), F32)
    attn_w_o_b = attn_w_o.astype(BF16)
    attn_qg = attn_q_g.reshape(nl_attn, 1, -1)
    attn_kg = attn_k_g.reshape(nl_attn, 1, -1)
    nkv = ATTN_KV_HEADS * ATTN_HEAD_DIM

    ffn_w_in_b = ffn_w_in.astype(BF16)
    ffn_w_out_b = ffn_w_out.astype(BF16)
    ffn_dw_b_r = ffn_dw_b.reshape(depth, 1, -1)
    pool_w_b = pool_w.astype(BF16)
    pool_scale_r = pool_scale.reshape(pool_scale.shape[0], 1, d)

    xs = [x_prompt.reshape(-1, d), x_sample.reshape(-1, d)]
    ssd_states, ctx_ks, ctx_vs = [], [], []
    for pi, p in enumerate(passes):
        x = xs[pi]
        is_ctx = pi == 0
        for i in range(depth):
            kind, j = i % 4, i // 4
            if kind == 0:
                x = _pool_sublayer(p, x, mods, norm_g, i, pool_w_b, pool_scale_r, j, tm=512)
            elif kind == 1:
                proj = _modin_matmul(p, x, mods, norm_g, i, ssd_w_in_b, ssd_zero_b, j, tm=512, tn=896)
                y, st = _ssd_core(p, proj.reshape(p.B, p.L, -1), ssd_conv_w, ssd_cb, ssd_dtb, ssd_alog, ssd_skip, j,
                                  None if is_ctx else state_ssd, is_ctx)
                if is_ctx:
                    ssd_states.append(st)
                x = _ssd_out(p, y.reshape(p.T, d_inner), proj, x, mods, norm_g, i, ssd_ng, ssd_w_out_b, j, tm=512)
            elif kind == 2:
                u = _modin_matmul(p, x, mods, norm_g, i, conf_w_in_b, conf_b_in_r, j, tm=512, tn=1024)
                x = _conf_sublayer(p, u, x, mods, norm_g, i, conf_dw, conf_vec(conf_dw_b), conf_vec(conf_ln_g),
                                   conf_vec(conf_ln_b), conf_w_out_b, conf_vec(conf_b_out), j, tm=512)
            else:
                qkv = _modin_matmul(p, x, mods, norm_g, i, attn_w_qkv_b, attn_zero_b, j, tm=512, tn=768)
                qn, kn, v = _qk_prep(p, qkv, attn_qg, attn_kg, j, not is_ctx, tm=512)
                r3 = lambda a: a.reshape(p.B, p.L, -1)
                if is_ctx:
                    ctx_ks.append(kn.reshape(p.B, p.L, ATTN_KV_HEADS, ATTN_HEAD_DIM))
                    ctx_vs.append(v.reshape(p.B, p.L, ATTN_KV_HEADS, ATTN_HEAD_DIM))
                    o = _attention(p, r3(qn), r3(kn), r3(v), None, None, 0, 0, tq=256)
                else:
                    nlp = cache_k.shape[1] * cache_k.shape[2]
                    kc = cache_k.reshape(p.B, nlp, nkv)
                    vc = cache_v.reshape(p.B, nlp, nkv)
                    o = _attention(p,r3(qn), r3(kn), r3(v), kc, vc, j, cache_k.shape[2], tq=256)
                x = _matmul_resid(p, o.reshape(p.T, -1), x, mods, norm_g, i, attn_w_o_b, j, tm=512)
            x = _ffn(p, x, mods, norm_g, i, ffn_w_in_b, ffn_dw, ffn_dw_b_r, ffn_w_out_b, tm=1024, tf=256)
        xs[pi] = x
    y_prompt = xs[0].reshape(x_prompt.shape)
    y_sample = xs[1].reshape(x_sample.shape)
    new_state = jnp.concatenate(ssd_states, axis=1)
    new_k = jnp.stack(ctx_ks, axis=1)
    new_v = jnp.stack(ctx_vs, axis=1)
    return (y_prompt, y_sample, new_state, new_k, new_v)
```
